```python
import jax, jax.numpy as jnp
from jax import lax
import numpy as np

D_MODEL = 1024
BATCH = 16
SEQ = 2048
DEPTH = 2

N_A_LAYERS = DEPTH // 2
N_B_LAYERS = DEPTH - N_A_LAYERS

CHUNK = 64
EPS = 1e-6
NEG_INF = -1e30

HEADS_A = 16
HEAD_DIM_A = D_MODEL // HEADS_A
LEFT_CHUNKS = 8
BAND = (LEFT_CHUNKS + 1) * CHUNK
MAX_REL = 128

HEADS_B = D_MODEL // 128
NOPE_DIM = 128
ROPE_DIM = 64
V_DIM = 128
Q_LORA = 768
KV_LORA = 256
ROPE_THETA = 10000.0
Q_BLOCK = 128

D_FF = ((8 * D_MODEL // 3 + 127) // 128) * 128

kernel_name = "yoco_chunked_relpos_mla_macaron"


def rms_norm(x, g):
    xf = x.astype(jnp.float32)
    y = xf * lax.rsqrt(jnp.mean(xf * xf, axis=-1, keepdims=True) + EPS)
    return (y * g.astype(jnp.float32)).astype(x.dtype)


def swiglu(h, w_in, w_out):
    u = h @ w_in
    return (jax.nn.silu(u[..., :D_FF]) * u[..., D_FF:]) @ w_out


def rope_tables(seq_len):
    half = ROPE_DIM // 2
    freqs = ROPE_THETA ** (-jnp.arange(half, dtype=jnp.float32) / half)
    ang = jnp.arange(seq_len, dtype=jnp.float32)[:, None] * freqs[None, :]
    return jnp.cos(ang), jnp.sin(ang)


def apply_rope(x, cos, sin):
    half = ROPE_DIM // 2
    cos = cos.astype(x.dtype)
    sin = sin.astype(x.dtype)
    x1, x2 = x[..., :half], x[..., half:]
    return jnp.concatenate([x1 * cos - x2 * sin, x2 * cos + x1 * sin], axis=-1)


def chunked_relpos_attention(h, w_qkv, w_o, rel_table):
    B, S, _ = h.shape
    nc = S // CHUNK
    qkv = (h @ w_qkv).reshape(B, S, 3, HEADS_A, HEAD_DIM_A)
    q, k, v = qkv[:, :, 0], qkv[:, :, 1], qkv[:, :, 2]
    pad = LEFT_CHUNKS * CHUNK
    kp = jnp.pad(k, ((0, 0), (pad, 0), (0, 0), (0, 0)))
    vp = jnp.pad(v, ((0, 0), (pad, 0), (0, 0), (0, 0)))
    qi = jnp.arange(CHUNK)[:, None]
    kj = jnp.arange(BAND)[None, :]
    rel_idx = jnp.clip(pad + qi - kj, -MAX_REL, MAX_REL) + MAX_REL
    bias = rel_table[:, rel_idx].astype(jnp.float32)
    q_chunks = jnp.moveaxis(q.reshape(B, nc, CHUNK, HEADS_A, HEAD_DIM_A), 1, 0)
    scale = HEAD_DIM_A ** -0.5

    def one_chunk(args):
        qc, c = args
        start = c * CHUNK
        kb = lax.dynamic_slice_in_dim(kp, start, BAND, axis=1)
        vb = lax.dynamic_slice_in_dim(vp, start, BAND, axis=1)
        s = jnp.einsum('bqhd,bkhd->bhqk', qc, kb).astype(jnp.float32) * scale + bias
        valid = kj >= pad - start
        s = jnp.where(valid[None, None], s, NEG_INF)
        p = jax.nn.softmax(s, axis=-1).astype(vb.dtype)
        return jnp.einsum('bhqk,bkhd->bqhd', p, vb)

    out = lax.map(one_chunk, (q_chunks, jnp.arange(nc)))
    out = jnp.moveaxis(out, 0, 1).reshape(B, S, HEADS_A * HEAD_DIM_A)
    return out @ w_o


def mla_shared_kv(h_kv, w_down, latent_norm, w_up, cos, sin):
    B, S, _ = h_kv.shape
    ckr = h_kv @ w_down
    c_kv = rms_norm(ckr[..., :KV_LORA], latent_norm)
    k_rope = apply_rope(ckr[..., KV_LORA:], cos, sin)
    kv = (c_kv @ w_up).reshape(B, S, HEADS_B, NOPE_DIM + V_DIM)
    return kv[..., :NOPE_DIM], k_rope, kv[..., NOPE_DIM:]


def mla_attention(h, w_dq, q_norm, w_uq, w_o, k_nope, k_rope, v, cos, sin):
    B, S, _ = h.shape
    cq = rms_norm(h @ w_dq, q_norm)
    q = (cq @ w_uq).reshape(B, S, HEADS_B, NOPE_DIM + ROPE_DIM)
    q_nope = q[..., :NOPE_DIM]
    q_rope = apply_rope(q[..., NOPE_DIM:], cos[:, None], sin[:, None])
    nb = S // Q_BLOCK
    qn_blocks = jnp.moveaxis(q_nope.reshape(B, nb, Q_BLOCK, HEADS_B, NOPE_DIM), 1, 0)
    qr_blocks = jnp.moveaxis(q_rope.reshape(B, nb, Q_BLOCK, HEADS_B, ROPE_DIM), 1, 0)
    key_chunk = jnp.arange(S) // CHUNK
    scale = (NOPE_DIM + ROPE_DIM) ** -0.5

    def one_block(args):
        qn, qr, bidx = args
        s = (jnp.einsum('bqhd,bkhd->bhqk', qn, k_nope)
             + jnp.einsum('bqhr,bkr->bhqk', qr, k_rope)).astype(jnp.float32) * scale
        q_chunk = (bidx * Q_BLOCK + jnp.arange(Q_BLOCK)) // CHUNK
        mask = key_chunk[None, :] <= q_chunk[:, None]
        s = jnp.where(mask[None, None], s, NEG_INF)
        p = jax.nn.softmax(s, axis=-1).astype(v.dtype)
        return jnp.einsum('bhqk,bkhd->bqhd', p, v)

    out = lax.map(one_block, (qn_blocks, qr_blocks, jnp.arange(nb)))
    out = jnp.moveaxis(out, 0, 1).reshape(B, S, HEADS_B * V_DIM)
    return out @ w_o


def setup_inputs(seed: int = 0) -> dict:
    key = jax.random.key(seed)
    ks = jax.random.split(key, 24)

    def w(k, shape, fan_in):
        return jax.random.normal(k, shape, jnp.float32) * fan_in ** -0.5

    def gain(k, shape):
        return 1.0 + 0.05 * jax.random.normal(k, shape, jnp.float32)

    return {
        "x": jax.random.normal(ks[0], (BATCH, SEQ, D_MODEL), jnp.float32),
        "ffn1_norm": gain(ks[1], (DEPTH, D_MODEL)),
        "ffn1_w_in": w(ks[2], (DEPTH, D_MODEL, 2 * D_FF), D_MODEL),
        "ffn1_w_out": w(ks[3], (DEPTH, D_FF, D_MODEL), D_FF),
        "mix_norm": gain(ks[4], (DEPTH, D_MODEL)),
        "ffn2_norm": gain(ks[5], (DEPTH, D_MODEL)),
        "ffn2_w_in": w(ks[6], (DEPTH, D_MODEL, 2 * D_FF), D_MODEL),
        "ffn2_w_out": w(ks[7], (DEPTH, D_FF, D_MODEL), D_FF),
        "a_w_qkv": w(ks[8], (N_A_LAYERS, D_MODEL, 3 * HEADS_A * HEAD_DIM_A), D_MODEL),
        "a_rel_bias": 0.5 * jax.random.normal(ks[9], (N_A_LAYERS, HEADS_A, 2 * MAX_REL + 1), jnp.float32),
        "a_w_o": w(ks[10], (N_A_LAYERS, HEADS_A * HEAD_DIM_A, D_MODEL), HEADS_A * HEAD_DIM_A),
        "kv_norm": gain(ks[11], (D_MODEL,)),
        "kv_w_down": w(ks[12], (D_MODEL, KV_LORA + ROPE_DIM), D_MODEL),
        "kv_latent_norm": gain(ks[13], (KV_LORA,)),
        "kv_w_up": w(ks[14], (KV_LORA, HEADS_B * (NOPE_DIM + V_DIM)), KV_LORA),
        "b_w_dq": w(ks[15], (N_B_LAYERS, D_MODEL, Q_LORA), D_MODEL),
        "b_q_norm": gain(ks[16], (N_B_LAYERS, Q_LORA)),
        "b_w_uq": w(ks[17], (N_B_LAYERS, Q_LORA, HEADS_B * (NOPE_DIM + ROPE_DIM)), Q_LORA),
        "b_w_o": w(ks[18], (N_B_LAYERS, HEADS_B * V_DIM, D_MODEL), HEADS_B * V_DIM),
        "final_norm": gain(ks[19], (D_MODEL,)),
    }


def reference(x, ffn1_norm, ffn1_w_in, ffn1_w_out, mix_norm, ffn2_norm, ffn2_w_in,
              ffn2_w_out, a_w_qkv, a_rel_bias, a_w_o, kv_norm, kv_w_down,
              kv_latent_norm, kv_w_up, b_w_dq, b_q_norm, b_w_uq, b_w_o, final_norm):
    S = x.shape[1]
    cos, sin = rope_tables(S)
    h = x
    k_nope = k_rope = v_shared = None
    for layer in range(DEPTH):
        h = h + 0.5 * swiglu(rms_norm(h, ffn1_norm[layer]), ffn1_w_in[layer], ffn1_w_out[layer])
        hn = rms_norm(h, mix_norm[layer])
        if layer < N_A_LAYERS:
            h = h + chunked_relpos_attention(hn, a_w_qkv[layer], a_w_o[layer], a_rel_bias[layer])
        else:
            li = layer - N_A_LAYERS
            h = h + mla_attention(hn, b_w_dq[li], b_q_norm[li], b_w_uq[li], b_w_o[li],
                                  k_nope, k_rope, v_shared, cos, sin)
        h = h + 0.5 * swiglu(rms_norm(h, ffn2_norm[layer]), ffn2_w_in[layer], ffn2_w_out[layer])
        if layer == N_A_LAYERS - 1:
            k_nope, k_rope, v_shared = mla_shared_kv(rms_norm(h, kv_norm), kv_w_down,
                                                     kv_latent_norm, kv_w_up, cos, sin)
    return rms_norm(h, final_norm)
```

```python
import functools

import jax
import jax.numpy as jnp
import numpy as np
from jax import lax
from jax.experimental import pallas as pl
from jax.experimental.pallas import tpu as pltpu

F32 = jnp.float32
BF16 = jnp.bfloat16

D_MODEL = 1024
CHUNK = 64
EPS = 1e-6
NEG_INF = -1e30

HEADS_A = 16
HEAD_DIM_A = 64
LEFT_CHUNKS = 8
MAX_REL = 128

HEADS_B = 8
NOPE_DIM = 128
ROPE_DIM = 64
V_DIM = 128
Q_LORA = 768
KV_LORA = 256
ROPE_THETA = 10000.0

D_FF = 2816
FF_CHUNK = 256
N_FF_CHUNKS = D_FF // FF_CHUNK

ROW_TILE = 512

A_QBLK = 4 * CHUNK
A_PAD = LEFT_CHUNKS * CHUNK
A_BAND = A_QBLK + A_PAD
LANES = 128

MLA_QBLK = 256

VMEM_LIMIT = 56 * 1024 * 1024


def _cparams(n_axes):
    return pltpu.CompilerParams(
        dimension_semantics=("arbitrary",) * n_axes, vmem_limit_bytes=VMEM_LIMIT)


def _rms(x, g):
    ms = jnp.mean(x * x, axis=-1, keepdims=True)
    return x * lax.rsqrt(ms + EPS) * g


def _resident(shape):
    zeros = (0,) * len(shape)
    return pl.BlockSpec(shape, lambda *_: zeros, pipeline_mode=pl.Buffered(1))


def _ffn_kernel(x_ref, g_ref, wgu_ref, wout_ref, *rest, final_norm):
    if final_norm:
        gf_ref, o_ref, hn_ref, acc_ref = rest
    else:
        o_ref, hn_ref, acc_ref = rest
    x = x_ref[...]
    hn_ref[...] = _rms(x, g_ref[...]).astype(BF16)

    def step(c, carry):
        u = jnp.dot(hn_ref[...], wgu_ref[c], preferred_element_type=F32)
        gate = u[:, :FF_CHUNK]
        up = u[:, FF_CHUNK:]
        act = (gate * jax.nn.sigmoid(gate) * up).astype(BF16)
        part = jnp.dot(act, wout_ref[c], preferred_element_type=F32)

        @pl.when(c == 0)
        def _():
            acc_ref[...] = part

        @pl.when(c > 0)
        def _():
            acc_ref[...] += part

        return carry

    lax.fori_loop(0, N_FF_CHUNKS, step, 0)
    y = x + 0.5 * acc_ref[...]
    if final_norm:
        y = _rms(y, gf_ref[...])
    o_ref[...] = y


def _ffn(h, g, wgu, wout, final_g=None):
    t = h.shape[0]
    row = pl.BlockSpec((ROW_TILE, D_MODEL), lambda i: (i, 0))
    in_specs = [row, _resident((1, D_MODEL)), _resident(wgu.shape), _resident(wout.shape)]
    args = [h, g, wgu, wout]
    if final_g is not None:
        in_specs.append(_resident((1, D_MODEL)))
        args.append(final_g)
    return pl.pallas_call(
        functools.partial(_ffn_kernel, final_norm=final_g is not None),
        grid=(t // ROW_TILE,),
        in_specs=in_specs,
        out_specs=row,
        out_shape=jax.ShapeDtypeStruct((t, D_MODEL), F32),
        scratch_shapes=[pltpu.VMEM((ROW_TILE, D_MODEL), BF16),
                        pltpu.VMEM((ROW_TILE, D_MODEL), F32)],
        compiler_params=_cparams(1),
        name="ffn",
    )(*args)


def _qkv_kernel(x_ref, g_ref, w_ref, q_ref, k_ref, v_ref):
    hn = _rms(x_ref[...], g_ref[...]).astype(BF16)
    qkv = jnp.dot(hn, w_ref[...], preferred_element_type=F32)
    q_ref[...] = (qkv[:, :D_MODEL] * (HEAD_DIM_A ** -0.5)).astype(BF16)
    k_ref[...] = qkv[:, D_MODEL:2 * D_MODEL].astype(BF16)
    v_ref[...] = qkv[:, 2 * D_MODEL:].astype(BF16)


def _qkv(h, g, w):
    t = h.shape[0]
    row = pl.BlockSpec((ROW_TILE, D_MODEL), lambda i: (i, 0))
    out = jax.ShapeDtypeStruct((t, D_MODEL), BF16)
    return pl.pallas_call(
        _qkv_kernel,
        grid=(t // ROW_TILE,),
        in_specs=[row, _resident((1, D_MODEL)), _resident(w.shape)],
        out_specs=[row, row, row],
        out_shape=[out, out, out],
        compiler_params=_cparams(1),
        name="qkv_a",
    )(h, g, w)


def _attn_a_kernel(q_ref, k_ref, v_ref, bias_ref, o_ref, kpad_ref, vpad_ref):
    seq = q_ref.shape[1]
    zeros = jnp.zeros((A_PAD, LANES), BF16)
    kpad_ref[:A_PAD, :] = zeros
    vpad_ref[:A_PAD, :] = zeros
    kpad_ref[A_PAD:, :] = k_ref[0]
    vpad_ref[A_PAD:, :] = v_ref[0]

    lane = lax.broadcasted_iota(jnp.int32, (1, LANES), 1)
    low_half = lane < HEAD_DIM_A
    col = lax.broadcasted_iota(jnp.int32, (1, A_BAND), 1)

    def block(blk, carry):
        start = pl.multiple_of(blk * A_QBLK, A_QBLK)
        q = q_ref[0, pl.ds(start, A_QBLK), :]
        kb = kpad_ref[pl.ds(start, A_BAND), :]
        vb = vpad_ref[pl.ds(start, A_BAND), :]
        valid = col >= A_PAD - start
        outs = []
        for half in range(2):
            keep = low_half if half == 0 else jnp.logical_not(low_half)
            qh = jnp.where(keep, q, jnp.zeros_like(q))
            s = lax.dot_general(qh, kb, (((1,), (1,)), ((), ())),
                                preferred_element_type=F32)
            s = jnp.where(valid, s + bias_ref[half], NEG_INF)
            m = jnp.max(s, axis=-1, keepdims=True)
            p = jnp.exp(s - m)
            l = jnp.sum(p, axis=-1, keepdims=True)
            o = jnp.dot(p.astype(BF16), vb, preferred_element_type=F32)
            outs.append(o / l)
        o_ref[0, pl.ds(start, A_QBLK), :] = jnp.where(low_half, outs[0], outs[1]).astype(BF16)
        return carry

    lax.fori_loop(0, seq // A_QBLK, block, 0)


def _attn_a(q, k, v, bias):
    b, s, _ = q.shape
    n_pairs = HEADS_A // 2
    blk = pl.BlockSpec((1, s, LANES), lambda p, i: (i, 0, p))
    return pl.pallas_call(
        _attn_a_kernel,
        grid=(n_pairs, b),
        in_specs=[blk, blk, blk,
                  pl.BlockSpec((2, A_QBLK, A_BAND), lambda p, i: (p, 0, 0))],
        out_specs=blk,
        out_shape=jax.ShapeDtypeStruct((b, s, D_MODEL), BF16),
        scratch_shapes=[pltpu.VMEM((A_PAD + s, LANES), BF16),
                        pltpu.VMEM((A_PAD + s, LANES), BF16)],
        compiler_params=_cparams(2),
        name="attn_a",
    )(q, k, v, bias)


def _attn_a_bias(rel_table):
    r = np.arange(A_QBLK)[:, None]
    j = np.arange(A_BAND)[None, :]
    rel_idx = np.clip(A_PAD + r - j, -MAX_REL, MAX_REL) + MAX_REL
    in_window = (j // CHUNK >= r // CHUNK) & (j // CHUNK <= r // CHUNK + LEFT_CHUNKS)
    bias = rel_table[:, rel_idx].astype(F32)
    return jnp.where(in_window[None], bias, NEG_INF)


def _proj_res_kernel(a_ref, w_ref, h_ref, o_ref):
    o_ref[...] = h_ref[...] + jnp.dot(a_ref[...], w_ref[...], preferred_element_type=F32)


def _proj_res(a, w, h):
    t, d = a.shape
    row = pl.BlockSpec((ROW_TILE, D_MODEL), lambda i: (i, 0))
    return pl.pallas_call(
        _proj_res_kernel,
        grid=(t // ROW_TILE,),
        in_specs=[pl.BlockSpec((ROW_TILE, d), lambda i: (i, 0)), _resident(w.shape), row],
        out_specs=row,
        out_shape=jax.ShapeDtypeStruct((t, D_MODEL), F32),
        compiler_params=_cparams(1),
        name="proj_res",
    )(a, w, h)


def _mla_kv_kernel(x_ref, g_ref, wd_ref, gl_ref, wk_ref, wv_ref, cos_ref, sin_ref,
                   kcat_ref, v_ref):
    hn = _rms(x_ref[...], g_ref[...]).astype(BF16)
    ckr = jnp.dot(hn, wd_ref[...], preferred_element_type=F32)
    c_kv = _rms(ckr[:, :KV_LORA], gl_ref[...]).astype(BF16)
    k_rope = (ckr[:, KV_LORA:KV_LORA + LANES] * cos_ref[...]
              + ckr[:, KV_LORA + LANES:] * sin_ref[...]).astype(BF16)
    k_nope = jnp.dot(c_kv, wk_ref[...], preferred_element_type=F32).astype(BF16)
    v_ref[...] = jnp.dot(c_kv, wv_ref[...], preferred_element_type=F32).astype(BF16)
    for hd in range(HEADS_B):
        kcat_ref[:, hd * 2 * LANES:hd * 2 * LANES + LANES] = k_nope[:, hd * LANES:(hd + 1) * LANES]
        kcat_ref[:, hd * 2 * LANES + LANES:(hd + 1) * 2 * LANES] = k_rope


def _mla_kv(h, g, wd, gl, wk, wv, cos2, sin2, seq):
    t = h.shape[0]
    n_seq = seq // ROW_TILE
    row = pl.BlockSpec((ROW_TILE, D_MODEL), lambda i: (i, 0))
    tab = pl.BlockSpec((ROW_TILE, LANES), lambda i: (i % n_seq, 0))
    return pl.pallas_call(
        _mla_kv_kernel,
        grid=(t // ROW_TILE,),
        in_specs=[row, _resident((1, D_MODEL)), _resident(wd.shape), _resident((1, KV_LORA)),
                  _resident(wk.shape), _resident(wv.shape), tab, tab],
        out_specs=[pl.BlockSpec((ROW_TILE, 2 * D_MODEL), lambda i: (i, 0)), row],
        out_shape=[jax.ShapeDtypeStruct((t, 2 * D_MODEL), BF16),
                   jax.ShapeDtypeStruct((t, D_MODEL), BF16)],
        compiler_params=_cparams(1),
        name="mla_kv",
    )(h, g, wd, gl, wk, wv, cos2, sin2)


def _mla_q_kernel(x_ref, g_ref, wdq_ref, gq_ref, wuq_ref, cos_ref, sin_ref, qcat_ref, *, scale):
    hn = _rms(x_ref[...], g_ref[...]).astype(BF16)
    cq = _rms(jnp.dot(hn, wdq_ref[...], preferred_element_type=F32), gq_ref[...]).astype(BF16)
    q = jnp.dot(cq, wuq_ref[...], preferred_element_type=F32)
    n_rope = HEADS_B * ROPE_DIM
    q_nope = q[:, :D_MODEL]
    q_rope = (q[:, D_MODEL:D_MODEL + n_rope] * cos_ref[...]
              + q[:, D_MODEL + n_rope:] * sin_ref[...])
    lane = lax.broadcasted_iota(jnp.int32, (1, LANES), 1)
    low_half = lane < ROPE_DIM
    for hd in range(HEADS_B):
        pair = q_rope[:, (hd // 2) * LANES:(hd // 2 + 1) * LANES]
        keep = low_half if hd % 2 == 0 else jnp.logical_not(low_half)
        qcat_ref[:, hd * 2 * LANES:hd * 2 * LANES + LANES] = (
            q_nope[:, hd * LANES:(hd + 1) * LANES] * scale).astype(BF16)
        qcat_ref[:, hd * 2 * LANES + LANES:(hd + 1) * 2 * LANES] = (
            jnp.where(keep, pair, 0.0) * scale).astype(BF16)


def _mla_q(h, g, wdq, gq, wuq, cos8, sin8, seq):
    t = h.shape[0]
    n_seq = seq // ROW_TILE
    n_rope = HEADS_B * ROPE_DIM
    row = pl.BlockSpec((ROW_TILE, D_MODEL), lambda i: (i, 0))
    tab = pl.BlockSpec((ROW_TILE, n_rope), lambda i: (i % n_seq, 0))
    scale = (NOPE_DIM + ROPE_DIM) ** -0.5
    return pl.pallas_call(
        functools.partial(_mla_q_kernel, scale=scale),
        grid=(t // ROW_TILE,),
        in_specs=[row, _resident((1, D_MODEL)), _resident(wdq.shape), _resident((1, Q_LORA)),
                  _resident(wuq.shape), tab, tab],
        out_specs=pl.BlockSpec((ROW_TILE, 2 * D_MODEL), lambda i: (i, 0)),
        out_shape=jax.ShapeDtypeStruct((t, 2 * D_MODEL), BF16),
        compiler_params=_cparams(1),
        name="mla_q",
    )(h, g, wdq, gq, wuq, cos8, sin8)


def _mla_attn_kernel(q_ref, k_ref, v_ref, o_ref):
    seq = q_ref.shape[1]
    n_blk = seq // MLA_QBLK
    r = lax.broadcasted_iota(jnp.int32, (MLA_QBLK, MLA_QBLK), 0)
    c = lax.broadcasted_iota(jnp.int32, (MLA_QBLK, MLA_QBLK), 1)
    diag_ok = (c // CHUNK) <= (r // CHUNK)

    def tile(q, j, m, l, acc, mask):
        ks = pl.multiple_of(j * MLA_QBLK, MLA_QBLK)
        kt = k_ref[0, pl.ds(ks, MLA_QBLK), :]
        vt = v_ref[0, pl.ds(ks, MLA_QBLK), :]
        s = lax.dot_general(q, kt, (((1,), (1,)), ((), ())), preferred_element_type=F32)
        if mask:
            s = jnp.where(diag_ok, s, NEG_INF)
        m_new = jnp.maximum(m, jnp.max(s, axis=-1, keepdims=True))
        alpha = jnp.exp(m - m_new)
        p = jnp.exp(s - m_new)
        l = alpha * l + jnp.sum(p, axis=-1, keepdims=True)
        acc = alpha * acc + jnp.dot(p.astype(BF16), vt, preferred_element_type=F32)
        return m_new, l, acc

    def qblock(i, carry):
        qs = pl.multiple_of(i * MLA_QBLK, MLA_QBLK)
        q = q_ref[0, pl.ds(qs, MLA_QBLK), :]
        m0 = jnp.full((MLA_QBLK, 1), NEG_INF, F32)
        l0 = jnp.zeros((MLA_QBLK, 1), F32)
        a0 = jnp.zeros((MLA_QBLK, V_DIM), F32)

        def inner(j, mla):
            return tile(q, j, *mla, mask=False)

        m, l, acc = lax.fori_loop(0, i, inner, (m0, l0, a0))
        m, l, acc = tile(q, i, m, l, acc, mask=True)
        o_ref[0, pl.ds(qs, MLA_QBLK), :] = (acc / l).astype(BF16)
        return carry

    lax.fori_loop(0, n_blk, qblock, 0)


def _mla_attn(qcat, kcat, v):
    b, s, _ = v.shape
    qk = pl.BlockSpec((1, s, 2 * LANES), lambda i, hd: (i, 0, hd))
    vo = pl.BlockSpec((1, s, V_DIM), lambda i, hd: (i, 0, hd))
    return pl.pallas_call(
        _mla_attn_kernel,
        grid=(b, HEADS_B),
        in_specs=[qk, qk, vo],
        out_specs=vo,
        out_shape=jax.ShapeDtypeStruct((b, s, HEADS_B * V_DIM), BF16),
        compiler_params=_cparams(2),
        name="mla_attn",
    )(qcat, kcat, v)


def _ffn_weights(w_in, w_out):
    d = w_in.shape[0]
    gate = w_in[:, :D_FF].reshape(d, N_FF_CHUNKS, FF_CHUNK)
    up = w_in[:, D_FF:].reshape(d, N_FF_CHUNKS, FF_CHUNK)
    wgu = jnp.concatenate([gate, up], axis=-1).transpose(1, 0, 2).astype(BF16)
    return wgu, w_out.reshape(N_FF_CHUNKS, FF_CHUNK, d).astype(BF16)


def _rope_tables(seq):
    half = ROPE_DIM // 2
    freqs = ROPE_THETA ** (-jnp.arange(half, dtype=F32) / half)
    ang = jnp.arange(seq, dtype=F32)[:, None] * freqs[None, :]
    cos, sin = jnp.cos(ang), jnp.sin(ang)
    return jnp.concatenate([cos, cos], axis=-1), jnp.concatenate([-sin, sin], axis=-1)


def kernel(x, ffn1_norm, ffn1_w_in, ffn1_w_out, mix_norm, ffn2_norm, ffn2_w_in, ffn2_w_out,
           a_w_qkv, a_rel_bias, a_w_o, kv_norm, kv_w_down, kv_latent_norm, kv_w_up,
           b_w_dq, b_q_norm, b_w_uq, b_w_o, final_norm):
    b, s, d = x.shape
    assert d == D_MODEL and s % ROW_TILE == 0 and s % A_QBLK == 0 and s % MLA_QBLK == 0
    t = b * s
    half = ROPE_DIM // 2
    row = lambda g: g.reshape(1, -1)

    cos64, sin64 = _rope_tables(s)
    swap = np.concatenate([np.arange(half, ROPE_DIM), np.arange(half)])

    h = x.reshape(t, d)
    h = _ffn(h, row(ffn1_norm[0]), *_ffn_weights(ffn1_w_in[0], ffn1_w_out[0]))
    q, k, v = _qkv(h, row(mix_norm[0]), a_w_qkv[0].astype(BF16))
    att = _attn_a(q.reshape(b, s, d), k.reshape(b, s, d), v.reshape(b, s, d),
                  _attn_a_bias(a_rel_bias[0]))
    h = _proj_res(att.reshape(t, d), a_w_o[0].astype(BF16), h)
    h = _ffn(h, row(ffn2_norm[0]), *_ffn_weights(ffn2_w_in[0], ffn2_w_out[0]))

    w_rope = kv_w_down[:, KV_LORA:]
    wd = jnp.concatenate([kv_w_down[:, :KV_LORA], w_rope, w_rope,
                          w_rope[:, swap], w_rope[:, swap]], axis=1).astype(BF16)
    w_up = kv_w_up.reshape(KV_LORA, HEADS_B, NOPE_DIM + V_DIM)
    wk = w_up[:, :, :NOPE_DIM].reshape(KV_LORA, HEADS_B * NOPE_DIM).astype(BF16)
    wv = w_up[:, :, NOPE_DIM:].reshape(KV_LORA, HEADS_B * V_DIM).astype(BF16)
    kcat, v_b = _mla_kv(h, row(kv_norm), wd, row(kv_latent_norm), wk, wv,
                        jnp.tile(cos64, (1, 2)), jnp.tile(sin64, (1, 2)), s)

    h = _ffn(h, row(ffn1_norm[1]), *_ffn_weights(ffn1_w_in[1], ffn1_w_out[1]))
    w_uq = b_w_uq[0].reshape(Q_LORA, HEADS_B, NOPE_DIM + ROPE_DIM)
    uq_nope = w_uq[:, :, :NOPE_DIM].reshape(Q_LORA, HEADS_B * NOPE_DIM)
    uq_rope = w_uq[:, :, NOPE_DIM:]
    wuq = jnp.concatenate([uq_nope, uq_rope.reshape(Q_LORA, -1),
                           uq_rope[:, :, swap].reshape(Q_LORA, -1)], axis=1).astype(BF16)
    qcat = _mla_q(h, row(mix_norm[1]), b_w_dq[0].astype(BF16), row(b_q_norm[0]), wuq,
                  jnp.tile(cos64, (1, HEADS_B)), jnp.tile(sin64, (1, HEADS_B)), s)
    att = _mla_attn(qcat.reshape(b, s, 2 * d), kcat.reshape(b, s, 2 * d), v_b.reshape(b, s, d))
    h = _proj_res(att.reshape(t, d), b_w_o[0].astype(BF16), h)
    h = _ffn(h, row(ffn2_norm[1]), *_ffn_weights(ffn2_w_in[1], ffn2_w_out[1]),
             final_g=row(final_norm))
    return h.reshape(b, s, d)
```

```python
import functools

import jax
import jax.numpy as jnp
import numpy as np
from jax import lax
from jax.experimental import pallas as pl
from jax.experimental.pallas import tpu as pltpu

F32 = jnp.float32
BF16 = jnp.bfloat16

D_MODEL = 1024
CHUNK = 64
EPS = 1e-6
NEG_INF = -1e30

HEADS_A = 16
HEAD_DIM_A = 64
LEFT_CHUNKS = 8
MAX_REL = 128

HEADS_B = 8
NOPE_DIM = 128
ROPE_DIM = 64
V_DIM = 128
Q_LORA = 768
KV_LORA = 256
ROPE_THETA = 10000.0

D_FF = 2816
FF_CHUNK = 256
N_FF_CHUNKS = D_FF // FF_CHUNK

ROW_TILE = 512

A_QBLK = 4 * CHUNK
A_PAD = LEFT_CHUNKS * CHUNK
A_BAND = A_QBLK + A_PAD
LANES = 128

MLA_QBLK = 256

VMEM_LIMIT = 56 * 1024 * 1024


def _cparams(n_axes):
    return pltpu.CompilerParams(
        dimension_semantics=("arbitrary",) * n_axes, vmem_limit_bytes=VMEM_LIMIT)


def _rms(x, g):
    ms = jnp.mean(x * x, axis=-1, keepdims=True)
    return x * lax.rsqrt(ms + EPS) * g


def _resident(shape):
    zeros = (0,) * len(shape)
    return pl.BlockSpec(shape, lambda *_: zeros, pipeline_mode=pl.Buffered(1))


def _ffn_kernel(x_ref, g_ref, wgu_ref, wout_ref, *rest, final_norm):
    if final_norm:
        gf_ref, o_ref, hn_ref = rest
    else:
        o_ref, hn_ref = rest
    x = x_ref[...]
    hn_ref[...] = _rms(x, g_ref[...]).astype(BF16)

    hn = hn_ref[...]
    acc = None
    for c in range(N_FF_CHUNKS):
        u = jnp.dot(hn, wgu_ref[c], preferred_element_type=F32)
        gate = u[:, :FF_CHUNK]
        up = u[:, FF_CHUNK:]
        act = (gate * jax.nn.sigmoid(gate) * up).astype(BF16)
        part = jnp.dot(act, wout_ref[c], preferred_element_type=F32)
        acc = part if acc is None else acc + part
    y = x + 0.5 * acc
    if final_norm:
        y = _rms(y, gf_ref[...])
    o_ref[...] = y


def _ffn(h, g, wgu, wout, final_g=None):
    t = h.shape[0]
    row = pl.BlockSpec((ROW_TILE, D_MODEL), lambda i: (i, 0))
    in_specs = [row, _resident((1, D_MODEL)), _resident(wgu.shape), _resident(wout.shape)]
    args = [h, g, wgu, wout]
    if final_g is not None:
        in_specs.append(_resident((1, D_MODEL)))
        args.append(final_g)
    return pl.pallas_call(
        functools.partial(_ffn_kernel, final_norm=final_g is not None),
        grid=(t // ROW_TILE,),
        in_specs=in_specs,
        out_specs=row,
        out_shape=jax.ShapeDtypeStruct((t, D_MODEL), F32),
        scratch_shapes=[pltpu.VMEM((ROW_TILE, D_MODEL), BF16)],
        compiler_params=_cparams(1),
        name="ffn",
    )(*args)


def _qkv_kernel(x_ref, g_ref, w_ref, q_ref, k_ref, v_ref):
    hn = _rms(x_ref[...], g_ref[...]).astype(BF16)
    qkv = jnp.dot(hn, w_ref[...], preferred_element_type=F32)
    q_ref[...] = (qkv[:, :D_MODEL] * (HEAD_DIM_A ** -0.5)).astype(BF16)
    k_ref[...] = qkv[:, D_MODEL:2 * D_MODEL].astype(BF16)
    v_ref[...] = qkv[:, 2 * D_MODEL:].astype(BF16)


def _qkv(h, g, w):
    t = h.shape[0]
    row = pl.BlockSpec((ROW_TILE, D_MODEL), lambda i: (i, 0))
    out = jax.ShapeDtypeStruct((t, D_MODEL), BF16)
    return pl.pallas_call(
        _qkv_kernel,
        grid=(t // ROW_TILE,),
        in_specs=[row, _resident((1, D_MODEL)), _resident(w.shape)],
        out_specs=[row, row, row],
        out_shape=[out, out, out],
        compiler_params=_cparams(1),
        name="qkv_a",
    )(h, g, w)


def _attn_a_kernel(q_ref, k_ref, v_ref, bias_ref, o_ref, kpad_ref, vpad_ref):
    seq = q_ref.shape[1]
    zeros = jnp.zeros((A_PAD, LANES), BF16)
    kpad_ref[:A_PAD, :] = zeros
    vpad_ref[:A_PAD, :] = zeros
    kpad_ref[A_PAD:, :] = k_ref[0]
    vpad_ref[A_PAD:, :] = v_ref[0]

    lane = lax.broadcasted_iota(jnp.int32, (1, LANES), 1)
    low_half = lane < HEAD_DIM_A
    col = lax.broadcasted_iota(jnp.int32, (1, A_BAND), 1)

    for blk in range(seq // A_QBLK):
        start = blk * A_QBLK
        q = q_ref[0, start:start + A_QBLK, :]
        kb = kpad_ref[start:start + A_BAND, :]
        vb = vpad_ref[start:start + A_BAND, :]
        outs = []
        for half in range(2):
            keep = low_half if half == 0 else jnp.logical_not(low_half)
            qh = jnp.where(keep, q, jnp.zeros_like(q))
            s = lax.dot_general(qh, kb, (((1,), (1,)), ((), ())),
                                preferred_element_type=F32)
            s = s + bias_ref[half]
            if start < A_PAD:
                s = jnp.where(col >= A_PAD - start, s, NEG_INF)
            m = jnp.max(s, axis=-1, keepdims=True)
            p = jnp.exp(s - m)
            l = jnp.sum(p, axis=-1, keepdims=True)
            o = jnp.dot(p.astype(BF16), vb, preferred_element_type=F32)
            outs.append(o / l)
        o_ref[0, start:start + A_QBLK, :] = jnp.where(low_half, outs[0], outs[1]).astype(BF16)


def _attn_a(q, k, v, bias):
    b, s, _ = q.shape
    n_pairs = HEADS_A // 2
    blk = pl.BlockSpec((1, s, LANES), lambda p, i: (i, 0, p))
    return pl.pallas_call(
        _attn_a_kernel,
        grid=(n_pairs, b),
        in_specs=[blk, blk, blk,
                  pl.BlockSpec((2, A_QBLK, A_BAND), lambda p, i: (p, 0, 0))],
        out_specs=blk,
        out_shape=jax.ShapeDtypeStruct((b, s, D_MODEL), BF16),
        scratch_shapes=[pltpu.VMEM((A_PAD + s, LANES), BF16),
                        pltpu.VMEM((A_PAD + s, LANES), BF16)],
        compiler_params=_cparams(2),
        name="attn_a",
    )(q, k, v, bias)


A_DIAG = 1024


def _attn_a_bias_kernel(g_ref, o_ref):
    base = jnp.broadcast_to(g_ref[0], (A_QBLK, A_DIAG))
    toeplitz = pltpu.roll(base, 0, 1, stride=1, stride_axis=0)[:, :A_BAND]
    r = lax.broadcasted_iota(jnp.int32, (A_QBLK, A_BAND), 0) // CHUNK
    j = lax.broadcasted_iota(jnp.int32, (A_QBLK, A_BAND), 1) // CHUNK
    in_window = jnp.logical_and(j >= r, j <= r + LEFT_CHUNKS)
    o_ref[0] = jnp.where(in_window, toeplitz, NEG_INF)


def _attn_a_bias(rel_table):
    n_heads = rel_table.shape[0]
    far = rel_table[:, 2 * MAX_REL:]
    n_far = A_PAD - MAX_REL
    n_near = A_BAND - n_far - (2 * MAX_REL + 1)
    g = jnp.concatenate([
        jnp.broadcast_to(far, (n_heads, n_far)),
        rel_table[:, ::-1],
        jnp.broadcast_to(rel_table[:, :1], (n_heads, n_near)),
        jnp.broadcast_to(far, (n_heads, A_DIAG - A_BAND)),
    ], axis=1).astype(F32).reshape(n_heads, 1, A_DIAG)
    return pl.pallas_call(
        _attn_a_bias_kernel,
        grid=(n_heads,),
        in_specs=[pl.BlockSpec((1, 1, A_DIAG), lambda hd: (hd, 0, 0))],
        out_specs=pl.BlockSpec((1, A_QBLK, A_BAND), lambda hd: (hd, 0, 0)),
        out_shape=jax.ShapeDtypeStruct((n_heads, A_QBLK, A_BAND), F32),
        compiler_params=_cparams(1),
        name="attn_a_bias",
    )(g)


def _proj_res_kernel(a_ref, w_ref, h_ref, o_ref):
    o_ref[...] = h_ref[...] + jnp.dot(a_ref[...], w_ref[...], preferred_element_type=F32)


def _proj_res(a, w, h):
    t, d = a.shape
    row = pl.BlockSpec((ROW_TILE, D_MODEL), lambda i: (i, 0))
    return pl.pallas_call(
        _proj_res_kernel,
        grid=(t // ROW_TILE,),
        in_specs=[pl.BlockSpec((ROW_TILE, d), lambda i: (i, 0)), _resident(w.shape), row],
        out_specs=row,
        out_shape=jax.ShapeDtypeStruct((t, D_MODEL), F32),
        compiler_params=_cparams(1),
        name="proj_res",
    )(a, w, h)


def _mla_kv_kernel(x_ref, g_ref, wd_ref, gl_ref, wk_ref, wv_ref, cos_ref, sin_ref,
                   kcat_ref, v_ref):
    hn = _rms(x_ref[...], g_ref[...]).astype(BF16)
    ckr = jnp.dot(hn, wd_ref[...], preferred_element_type=F32)
    c_kv = _rms(ckr[:, :KV_LORA], gl_ref[...]).astype(BF16)
    k_rope = (ckr[:, KV_LORA:KV_LORA + LANES] * cos_ref[...]
              + ckr[:, KV_LORA + LANES:] * sin_ref[...]).astype(BF16)
    k_nope = jnp.dot(c_kv, wk_ref[...], preferred_element_type=F32).astype(BF16)
    v_ref[...] = jnp.dot(c_kv, wv_ref[...], preferred_element_type=F32).astype(BF16)
    for hd in range(HEADS_B):
        kcat_ref[:, hd * 2 * LANES:hd * 2 * LANES + LANES] = k_nope[:, hd * LANES:(hd + 1) * LANES]
        kcat_ref[:, hd * 2 * LANES + LANES:(hd + 1) * 2 * LANES] = k_rope


def _mla_kv(h, g, wd, gl, wk, wv, cos2, sin2, seq):
    t = h.shape[0]
    n_seq = seq // ROW_TILE
    row = pl.BlockSpec((ROW_TILE, D_MODEL), lambda i: (i, 0))
    tab = pl.BlockSpec((ROW_TILE, LANES), lambda i: (i % n_seq, 0))
    return pl.pallas_call(
        _mla_kv_kernel,
        grid=(t // ROW_TILE,),
        in_specs=[row, _resident((1, D_MODEL)), _resident(wd.shape), _resident((1, KV_LORA)),
                  _resident(wk.shape), _resident(wv.shape), tab, tab],
        out_specs=[pl.BlockSpec((ROW_TILE, 2 * D_MODEL), lambda i: (i, 0)), row],
        out_shape=[jax.ShapeDtypeStruct((t, 2 * D_MODEL), BF16),
                   jax.ShapeDtypeStruct((t, D_MODEL), BF16)],
        compiler_params=_cparams(1),
        name="mla_kv",
    )(h, g, wd, gl, wk, wv, cos2, sin2)


def _mla_q_kernel(x_ref, g_ref, wdq_ref, gq_ref, wuq_ref, cos_ref, sin_ref, qcat_ref, *, scale):
    hn = _rms(x_ref[...], g_ref[...]).astype(BF16)
    cq = _rms(jnp.dot(hn, wdq_ref[...], preferred_element_type=F32), gq_ref[...]).astype(BF16)
    q = jnp.dot(cq, wuq_ref[...], preferred_element_type=F32)
    n_rope = HEADS_B * ROPE_DIM
    q_nope = q[:, :D_MODEL]
    q_rope = (q[:, D_MODEL:D_MODEL + n_rope] * cos_ref[...]
              + q[:, D_MODEL + n_rope:] * sin_ref[...])
    lane = lax.broadcasted_iota(jnp.int32, (1, LANES), 1)
    low_half = lane < ROPE_DIM
    for hd in range(HEADS_B):
        pair = q_rope[:, (hd // 2) * LANES:(hd // 2 + 1) * LANES]
        keep = low_half if hd % 2 == 0 else jnp.logical_not(low_half)
        qcat_ref[:, hd * 2 * LANES:hd * 2 * LANES + LANES] = (
            q_nope[:, hd * LANES:(hd + 1) * LANES] * scale).astype(BF16)
        qcat_ref[:, hd * 2 * LANES + LANES:(hd + 1) * 2 * LANES] = (
            jnp.where(keep, pair, 0.0) * scale).astype(BF16)


def _mla_q(h, g, wdq, gq, wuq, cos8, sin8, seq):
    t = h.shape[0]
    n_seq = seq // ROW_TILE
    n_rope = HEADS_B * ROPE_DIM
    row = pl.BlockSpec((ROW_TILE, D_MODEL), lambda i: (i, 0))
    tab = pl.BlockSpec((ROW_TILE, n_rope), lambda i: (i % n_seq, 0))
    scale = (NOPE_DIM + ROPE_DIM) ** -0.5
    return pl.pallas_call(
        functools.partial(_mla_q_kernel, scale=scale),
        grid=(t // ROW_TILE,),
        in_specs=[row, _resident((1, D_MODEL)), _resident(wdq.shape), _resident((1, Q_LORA)),
                  _resident(wuq.shape), tab, tab],
        out_specs=pl.BlockSpec((ROW_TILE, 2 * D_MODEL), lambda i: (i, 0)),
        out_shape=jax.ShapeDtypeStruct((t, 2 * D_MODEL), BF16),
        compiler_params=_cparams(1),
        name="mla_q",
    )(h, g, wdq, gq, wuq, cos8, sin8)


def _mla_attn_kernel(q_ref, k_ref, v_ref, o_ref):
    seq = q_ref.shape[1]
    r = lax.broadcasted_iota(jnp.int32, (MLA_QBLK, MLA_QBLK), 0)
    c = lax.broadcasted_iota(jnp.int32, (MLA_QBLK, MLA_QBLK), 1)
    diag_ok = (c // CHUNK) <= (r // CHUNK)

    for i in range(seq // MLA_QBLK):
        lo, end = i * MLA_QBLK, (i + 1) * MLA_QBLK
        q = q_ref[0, lo:end, :]
        s_diag = lax.dot_general(q, k_ref[0, lo:end, :], (((1,), (1,)), ((), ())),
                                 preferred_element_type=F32)
        s_diag = jnp.where(diag_ok, s_diag, NEG_INF)
        m = jnp.max(s_diag, axis=-1, keepdims=True)
        if i > 0:
            s_past = lax.dot_general(q, k_ref[0, :lo, :], (((1,), (1,)), ((), ())),
                                     preferred_element_type=F32)
            m = jnp.maximum(m, jnp.max(s_past, axis=-1, keepdims=True))
            p = jnp.concatenate([jnp.exp(s_past - m), jnp.exp(s_diag - m)], axis=-1)
        else:
            p = jnp.exp(s_diag - m)
        l = jnp.sum(p, axis=-1, keepdims=True)
        o = jnp.dot(p.astype(BF16), v_ref[0, :end, :], preferred_element_type=F32)
        o_ref[0, lo:end, :] = (o / l).astype(BF16)


def _mla_attn(qcat, kcat, v):
    b, s, _ = v.shape
    qk = pl.BlockSpec((1, s, 2 * LANES), lambda i, hd: (i, 0, hd))
    vo = pl.BlockSpec((1, s, V_DIM), lambda i, hd: (i, 0, hd))
    return pl.pallas_call(
        _mla_attn_kernel,
        grid=(b, HEADS_B),
        in_specs=[qk, qk, vo],
        out_specs=vo,
        out_shape=jax.ShapeDtypeStruct((b, s, HEADS_B * V_DIM), BF16),
        compiler_params=_cparams(2),
        name="mla_attn",
    )(qcat, kcat, v)


def _ffn_weights(w_in, w_out):
    d = w_in.shape[0]
    gate = w_in[:, :D_FF].reshape(d, N_FF_CHUNKS, FF_CHUNK)
    up = w_in[:, D_FF:].reshape(d, N_FF_CHUNKS, FF_CHUNK)
    wgu = jnp.concatenate([gate, up], axis=-1).transpose(1, 0, 2).astype(BF16)
    return wgu, w_out.reshape(N_FF_CHUNKS, FF_CHUNK, d).astype(BF16)


def _rope_tables(seq):
    half = ROPE_DIM // 2
    freqs = ROPE_THETA ** (-jnp.arange(half, dtype=F32) / half)
    ang = jnp.arange(seq, dtype=F32)[:, None] * freqs[None, :]
    cos, sin = jnp.cos(ang), jnp.sin(ang)
    return jnp.concatenate([cos, cos], axis=-1), jnp.concatenate([-sin, sin], axis=-1)


def kernel(x, ffn1_norm, ffn1_w_in, ffn1_w_out, mix_norm, ffn2_norm, ffn2_w_in, ffn2_w_out,
           a_w_qkv, a_rel_bias, a_w_o, kv_norm, kv_w_down, kv_latent_norm, kv_w_up,
           b_w_dq, b_q_norm, b_w_uq, b_w_o, final_norm):
    b, s, d = x.shape
    assert d == D_MODEL and s % ROW_TILE == 0 and s % A_QBLK == 0 and s % MLA_QBLK == 0
    t = b * s
    half = ROPE_DIM // 2
    row = lambda g: g.reshape(1, -1)

    cos64, sin64 = _rope_tables(s)
    swap = np.concatenate([np.arange(half, ROPE_DIM), np.arange(half)])

    h = x.reshape(t, d)
    h = _ffn(h, row(ffn1_norm[0]), *_ffn_weights(ffn1_w_in[0], ffn1_w_out[0]))
    q, k, v = _qkv(h, row(mix_norm[0]), a_w_qkv[0].astype(BF16))
    att = _attn_a(q.reshape(b, s, d), k.reshape(b, s, d), v.reshape(b, s, d),
                  _attn_a_bias(a_rel_bias[0]))
    h = _proj_res(att.reshape(t, d), a_w_o[0].astype(BF16), h)
    h = _ffn(h, row(ffn2_norm[0]), *_ffn_weights(ffn2_w_in[0], ffn2_w_out[0]))

    w_rope = kv_w_down[:, KV_LORA:]
    wd = jnp.concatenate([kv_w_down[:, :KV_LORA], w_rope, w_rope,
                          w_rope[:, swap], w_rope[:, swap]], axis=1).astype(BF16)
    w_up = kv_w_up.reshape(KV_LORA, HEADS_B, NOPE_DIM + V_DIM)
    wk = w_up[:, :, :NOPE_DIM].reshape(KV_LORA, HEADS_B * NOPE_DIM).astype(BF16)
    wv = w_up[:, :, NOPE_DIM:].reshape(KV_LORA, HEADS_B * V_DIM).astype(BF16)
    kcat, v_b = _mla_kv(h, row(kv_norm), wd, row(kv_latent_norm), wk, wv,
                        jnp.tile(cos64, (1, 2)), jnp.tile(sin64, (1, 2)), s)

    h = _ffn(h, row(ffn1_norm[1]), *_ffn_weights(ffn1_w_in[1], ffn1_w_out[1]))
    w_uq = b_w_uq[0].reshape(Q_LORA, HEADS_B, NOPE_DIM + ROPE_DIM)
    uq_nope = w_uq[:, :, :NOPE_DIM].reshape(Q_LORA, HEADS_B * NOPE_DIM)
    uq_rope = w_uq[:, :, NOPE_DIM:]
    wuq = jnp.concatenate([uq_nope, uq_rope.reshape(Q_LORA, -1),
                           uq_rope[:, :, swap].reshape(Q_LORA, -1)], axis=1).astype(BF16)
    qcat = _mla_q(h, row(mix_norm[1]), b_w_dq[0].astype(BF16), row(b_q_norm[0]), wuq,
                  jnp.tile(cos64, (1, HEADS_B)), jnp.tile(sin64, (1, HEADS_B)), s)
    att = _mla_attn(qcat.reshape(b, s, 2 * d), kcat.reshape(b, s, 2 * d), v_b.reshape(b, s, d))
    h = _proj_res(att.reshape(t, d), b_w_o[0].astype(BF16), h)
    h = _ffn(h, row(ffn2_norm[1]), *_ffn_weights(ffn2_w_in[1], ffn2_w_out[1]),
             final_g=row(final_norm))
    return h.reshape(b, s, d)
```

```python
import functools

import jax
import jax.numpy as jnp
import numpy as np
from jax import lax
from jax.experimental import pallas as pl
from jax.experimental.pallas import tpu as pltpu

F32 = jnp.float32
BF16 = jnp.bfloat16

D_MODEL = 1024
CHUNK = 64
EPS = 1e-6
NEG_INF = -1e30
LOG2E = 1.4426950408889634

HEADS_A = 16
HEAD_DIM_A = 64
LEFT_CHUNKS = 8
MAX_REL = 128

HEADS_B = 8
NOPE_DIM = 128
ROPE_DIM = 64
V_DIM = 128
Q_LORA = 768
KV_LORA = 256
ROPE_THETA = 10000.0

D_FF = 2816
FF_CHUNK = 256
N_FF_CHUNKS = D_FF // FF_CHUNK

ROW_TILE = 512

A_QBLK = 4 * CHUNK
A_PAD = LEFT_CHUNKS * CHUNK
A_BAND = A_QBLK + A_PAD
LANES = 128

MLA_QBLK = 256

VMEM_LIMIT = 56 * 1024 * 1024


def _cparams(n_axes):
    return pltpu.CompilerParams(
        dimension_semantics=("arbitrary",) * n_axes, vmem_limit_bytes=VMEM_LIMIT)


def _rms(x, g):
    ms = jnp.mean(x * x, axis=-1, keepdims=True)
    return x * lax.rsqrt(ms + EPS) * g


def _resident(shape):
    zeros = (0,) * len(shape)
    return pl.BlockSpec(shape, lambda *_: zeros, pipeline_mode=pl.Buffered(1))


def _ffn_kernel(*refs, mixer_proj, final_norm):
    refs = list(refs)
    x_ref = refs.pop(0)
    if mixer_proj:
        a_ref, wo_ref = refs.pop(0), refs.pop(0)
    g_ref, win_ref, wout_ref = refs.pop(0), refs.pop(0), refs.pop(0)
    if final_norm:
        gf_ref = refs.pop(0)
    o_ref, hn_ref = refs
    x = x_ref[...]
    if mixer_proj:
        x = x + jnp.dot(a_ref[...], wo_ref[...], preferred_element_type=F32)
        o_ref[...] = x
    hn_ref[...] = _rms(x, g_ref[...]).astype(BF16)

    hn = hn_ref[...]
    acc = None
    for c in range(N_FF_CHUNKS):
        lo = c * FF_CHUNK
        gate = jnp.dot(hn, win_ref[:, lo:lo + FF_CHUNK], preferred_element_type=F32)
        up = jnp.dot(hn, win_ref[:, D_FF + lo:D_FF + lo + FF_CHUNK], preferred_element_type=F32)
        act = (gate * jax.nn.sigmoid(gate) * up).astype(BF16)
        part = jnp.dot(act, wout_ref[lo:lo + FF_CHUNK, :], preferred_element_type=F32)
        acc = part if acc is None else acc + part
    y = (o_ref[...] if mixer_proj else x_ref[...]) + 0.5 * acc
    if final_norm:
        y = _rms(y, gf_ref[...])
    o_ref[...] = y


def _ffn(h, g, w_in, w_out, attn=None, w_o=None, final_g=None):
    t = h.shape[0]
    row = pl.BlockSpec((ROW_TILE, D_MODEL), lambda i: (i, 0))
    in_specs, args = [row], [h]
    if attn is not None:
        in_specs += [row, _resident(w_o.shape)]
        args += [attn, w_o]
    in_specs += [_resident((1, D_MODEL)), _resident(w_in.shape), _resident(w_out.shape)]
    args += [g, w_in, w_out]
    if final_g is not None:
        in_specs.append(_resident((1, D_MODEL)))
        args.append(final_g)
    return pl.pallas_call(
        functools.partial(_ffn_kernel, mixer_proj=attn is not None,
                          final_norm=final_g is not None),
        grid=(t // ROW_TILE,),
        in_specs=in_specs,
        out_specs=row,
        out_shape=jax.ShapeDtypeStruct((t, D_MODEL), F32),
        scratch_shapes=[pltpu.VMEM((ROW_TILE, D_MODEL), BF16)],
        compiler_params=_cparams(1),
        name="ffn",
    )(*args)


def _qkv_kernel(x_ref, g_ref, w_ref, q_ref, k_ref, v_ref):
    hn = _rms(x_ref[...], g_ref[...]).astype(BF16)
    qkv = jnp.dot(hn, w_ref[...], preferred_element_type=F32)
    q_ref[...] = (qkv[:, :D_MODEL] * (HEAD_DIM_A ** -0.5 * LOG2E)).astype(BF16)
    k_ref[...] = qkv[:, D_MODEL:2 * D_MODEL].astype(BF16)
    v_ref[...] = qkv[:, 2 * D_MODEL:].astype(BF16)


def _qkv(h, g, w):
    t = h.shape[0]
    row = pl.BlockSpec((ROW_TILE, D_MODEL), lambda i: (i, 0))
    out = jax.ShapeDtypeStruct((t, D_MODEL), BF16)
    return pl.pallas_call(
        _qkv_kernel,
        grid=(t // ROW_TILE,),
        in_specs=[row, _resident((1, D_MODEL)), _resident(w.shape)],
        out_specs=[row, row, row],
        out_shape=[out, out, out],
        compiler_params=_cparams(1),
        name="qkv_a",
    )(h, g, w)


def _attn_a_kernel(q_ref, k_ref, v_ref, bias_ref, o_ref):
    seq = q_ref.shape[1]
    lane = lax.broadcasted_iota(jnp.int32, (1, LANES), 1)
    low_half = lane < HEAD_DIM_A

    for blk in range(seq // A_QBLK):
        start = blk * A_QBLK
        key_lo = max(start - A_PAD, 0)
        col_lo = key_lo - (start - A_PAD)
        q = q_ref[0, start:start + A_QBLK, :]
        zero = jnp.zeros_like(q)
        q2 = jnp.concatenate([jnp.where(low_half, q, zero),
                              jnp.where(low_half, zero, q)], axis=0)
        kb = k_ref[0, key_lo:start + A_QBLK, :]
        vb = v_ref[0, key_lo:start + A_QBLK, :]
        s = lax.dot_general(q2, kb, (((1,), (1,)), ((), ())),
                            preferred_element_type=F32)
        s = s + bias_ref[0, :, col_lo:]
        m = jnp.max(s, axis=-1, keepdims=True)
        p = jnp.exp2(s - m)
        l = jnp.sum(p, axis=-1, keepdims=True)
        o2 = jnp.dot(p.astype(BF16), vb, preferred_element_type=F32) / l
        o_ref[0, start:start + A_QBLK, :] = jnp.where(
            low_half, o2[:A_QBLK], o2[A_QBLK:]).astype(BF16)


def _attn_a(q, k, v, bias):
    b, s, _ = q.shape
    n_pairs = HEADS_A // 2
    blk = pl.BlockSpec((1, s, LANES), lambda p, i: (i, 0, p))
    return pl.pallas_call(
        _attn_a_kernel,
        grid=(n_pairs, b),
        in_specs=[blk, blk, blk,
                  pl.BlockSpec((1, 2 * A_QBLK, A_BAND), lambda p, i: (p, 0, 0))],
        out_specs=blk,
        out_shape=jax.ShapeDtypeStruct((b, s, D_MODEL), BF16),
        compiler_params=_cparams(2),
        name="attn_a",
    )(q, k, v, bias.reshape(n_pairs, 2 * A_QBLK, A_BAND))


A_DIAG = 1024


def _attn_a_bias_kernel(g_ref, o_ref):
    base = jnp.broadcast_to(g_ref[0], (A_QBLK, A_DIAG))
    toeplitz = pltpu.roll(base, 0, 1, stride=1, stride_axis=0)[:, :A_BAND]
    r = lax.broadcasted_iota(jnp.int32, (A_QBLK, A_BAND), 0) // CHUNK
    j = lax.broadcasted_iota(jnp.int32, (A_QBLK, A_BAND), 1) // CHUNK
    in_window = jnp.logical_and(j >= r, j <= r + LEFT_CHUNKS)
    o_ref[0] = jnp.where(in_window, toeplitz * LOG2E, NEG_INF)


def _attn_a_bias(rel_table):
    n_heads = rel_table.shape[0]
    far = rel_table[:, 2 * MAX_REL:]
    n_far = A_PAD - MAX_REL
    n_near = A_BAND - n_far - (2 * MAX_REL + 1)
    g = jnp.concatenate([
        jnp.broadcast_to(far, (n_heads, n_far)),
        rel_table[:, ::-1],
        jnp.broadcast_to(rel_table[:, :1], (n_heads, n_near)),
        jnp.broadcast_to(far, (n_heads, A_DIAG - A_BAND)),
    ], axis=1).astype(F32).reshape(n_heads, 1, A_DIAG)
    return pl.pallas_call(
        _attn_a_bias_kernel,
        grid=(n_heads,),
        in_specs=[pl.BlockSpec((1, 1, A_DIAG), lambda hd: (hd, 0, 0))],
        out_specs=pl.BlockSpec((1, A_QBLK, A_BAND), lambda hd: (hd, 0, 0)),
        out_shape=jax.ShapeDtypeStruct((n_heads, A_QBLK, A_BAND), F32),
        compiler_params=_cparams(1),
        name="attn_a_bias",
    )(g)


def _mla_kv_kernel(x_ref, g_ref, wd_ref, gl_ref, wk_ref, wv_ref, cos_ref, sin_ref,
                   kcat_ref, v_ref):
    hn = _rms(x_ref[...], g_ref[...]).astype(BF16)
    ckr = jnp.dot(hn, wd_ref[...], preferred_element_type=F32)
    c_kv = _rms(ckr[:, :KV_LORA], gl_ref[...]).astype(BF16)
    k_rope = (ckr[:, KV_LORA:KV_LORA + LANES] * cos_ref[...]
              + ckr[:, KV_LORA + LANES:] * sin_ref[...]).astype(BF16)
    k_nope = jnp.dot(c_kv, wk_ref[...], preferred_element_type=F32).astype(BF16)
    v_ref[...] = jnp.dot(c_kv, wv_ref[...], preferred_element_type=F32).astype(BF16)
    for hd in range(HEADS_B):
        kcat_ref[:, hd * 2 * LANES:hd * 2 * LANES + LANES] = k_nope[:, hd * LANES:(hd + 1) * LANES]
        kcat_ref[:, hd * 2 * LANES + LANES:(hd + 1) * 2 * LANES] = k_rope


def _mla_kv(h, g, wd, gl, wk, wv, cos2, sin2, seq):
    t = h.shape[0]
    n_seq = seq // ROW_TILE
    row = pl.BlockSpec((ROW_TILE, D_MODEL), lambda i: (i, 0))
    tab = pl.BlockSpec((ROW_TILE, LANES), lambda i: (i % n_seq, 0))
    return pl.pallas_call(
        _mla_kv_kernel,
        grid=(t // ROW_TILE,),
        in_specs=[row, _resident((1, D_MODEL)), _resident(wd.shape), _resident((1, KV_LORA)),
                  _resident(wk.shape), _resident(wv.shape), tab, tab],
        out_specs=[pl.BlockSpec((ROW_TILE, 2 * D_MODEL), lambda i: (i, 0)), row],
        out_shape=[jax.ShapeDtypeStruct((t, 2 * D_MODEL), BF16),
                   jax.ShapeDtypeStruct((t, D_MODEL), BF16)],
        compiler_params=_cparams(1),
        name="mla_kv",
    )(h, g, wd, gl, wk, wv, cos2, sin2)


def _mla_q_kernel(x_ref, g_ref, wdq_ref, gq_ref, wuq_ref, cos_ref, sin_ref, qcat_ref, *, scale):
    hn = _rms(x_ref[...], g_ref[...]).astype(BF16)
    cq = _rms(jnp.dot(hn, wdq_ref[...], preferred_element_type=F32), gq_ref[...]).astype(BF16)
    q = jnp.dot(cq, wuq_ref[...], preferred_element_type=F32)
    n_rope = HEADS_B * ROPE_DIM
    q_nope = q[:, :D_MODEL]
    q_rope = (q[:, D_MODEL:D_MODEL + n_rope] * cos_ref[...]
              + q[:, D_MODEL + n_rope:] * sin_ref[...])
    lane = lax.broadcasted_iota(jnp.int32, (1, LANES), 1)
    low_half = lane < ROPE_DIM
    for hd in range(HEADS_B):
        pair = q_rope[:, (hd // 2) * LANES:(hd // 2 + 1) * LANES]
        keep = low_half if hd % 2 == 0 else jnp.logical_not(low_half)
        qcat_ref[:, hd * 2 * LANES:hd * 2 * LANES + LANES] = (
            q_nope[:, hd * LANES:(hd + 1) * LANES] * scale).astype(BF16)
        qcat_ref[:, hd * 2 * LANES + LANES:(hd + 1) * 2 * LANES] = (
            jnp.where(keep, pair, 0.0) * scale).astype(BF16)


def _mla_q(h, g, wdq, gq, wuq, cos8, sin8, seq):
    t = h.shape[0]
    n_seq = seq // ROW_TILE
    n_rope = HEADS_B * ROPE_DIM
    row = pl.BlockSpec((ROW_TILE, D_MODEL), lambda i: (i, 0))
    tab = pl.BlockSpec((ROW_TILE, n_rope), lambda i: (i % n_seq, 0))
    scale = (NOPE_DIM + ROPE_DIM) ** -0.5 * LOG2E
    return pl.pallas_call(
        functools.partial(_mla_q_kernel, scale=scale),
        grid=(t // ROW_TILE,),
        in_specs=[row, _resident((1, D_MODEL)), _resident(wdq.shape), _resident((1, Q_LORA)),
                  _resident(wuq.shape), tab, tab],
        out_specs=pl.BlockSpec((ROW_TILE, 2 * D_MODEL), lambda i: (i, 0)),
        out_shape=jax.ShapeDtypeStruct((t, 2 * D_MODEL), BF16),
        compiler_params=_cparams(1),
        name="mla_q",
    )(h, g, wdq, gq, wuq, cos8, sin8)


def _mla_attn_kernel(q_ref, k_ref, v_ref, o_ref):
    seq = q_ref.shape[1]
    n_blk = seq // MLA_QBLK
    r = lax.broadcasted_iota(jnp.int32, (MLA_QBLK, MLA_QBLK), 0)
    c = lax.broadcasted_iota(jnp.int32, (MLA_QBLK, MLA_QBLK), 1)
    diag_ok = (c // CHUNK) <= (r // CHUNK)

    m = [None] * n_blk
    l = [None] * n_blk
    acc = [None] * n_blk
    for j in range(n_blk):
        lo = j * MLA_QBLK
        kt = k_ref[0, lo:lo + MLA_QBLK, :]
        vt = v_ref[0, lo:lo + MLA_QBLK, :]
        s = lax.dot_general(q_ref[0, lo:, :], kt, (((1,), (1,)), ((), ())),
                            preferred_element_type=F32)
        probs, alphas = [], []
        for i in range(j, n_blk):
            si = s[(i - j) * MLA_QBLK:(i - j + 1) * MLA_QBLK]
            if i == j:
                si = jnp.where(diag_ok, si, NEG_INF)
            mi = jnp.max(si, axis=-1, keepdims=True)
            if j == 0:
                alpha = None
            else:
                mi = jnp.maximum(m[i], mi)
                alpha = jnp.exp2(m[i] - mi)
            p = jnp.exp2(si - mi)
            li = p[:, :LANES] + p[:, LANES:]
            l[i] = li if j == 0 else alpha * l[i] + li
            m[i] = mi
            probs.append(p.astype(BF16))
            alphas.append(alpha)
        pv = jnp.dot(jnp.concatenate(probs, axis=0), vt, preferred_element_type=F32)
        for i in range(j, n_blk):
            pvi = pv[(i - j) * MLA_QBLK:(i - j + 1) * MLA_QBLK]
            acc[i] = pvi if j == 0 else alphas[i - j] * acc[i] + pvi
        denom = jnp.sum(l[j], axis=-1, keepdims=True)
        o_ref[0, lo:lo + MLA_QBLK, :] = (acc[j] / denom).astype(BF16)


def _mla_attn(qcat, kcat, v):
    b, s, _ = v.shape
    qk = pl.BlockSpec((1, s, 2 * LANES), lambda i, hd: (i, 0, hd))
    vo = pl.BlockSpec((1, s, V_DIM), lambda i, hd: (i, 0, hd))
    return pl.pallas_call(
        _mla_attn_kernel,
        grid=(b, HEADS_B),
        in_specs=[qk, qk, vo],
        out_specs=vo,
        out_shape=jax.ShapeDtypeStruct((b, s, HEADS_B * V_DIM), BF16),
        compiler_params=_cparams(2),
        name="mla_attn",
    )(qcat, kcat, v)


def _ffn_weights(w_in, w_out):
    return w_in.astype(BF16), w_out.astype(BF16)


def _rope_tables(seq):
    half = ROPE_DIM // 2
    freqs = ROPE_THETA ** (-jnp.arange(half, dtype=F32) / half)
    ang = jnp.arange(seq, dtype=F32)[:, None] * freqs[None, :]
    cos, sin = jnp.cos(ang), jnp.sin(ang)
    return jnp.concatenate([cos, cos], axis=-1), jnp.concatenate([-sin, sin], axis=-1)


def kernel(x, ffn1_norm, ffn1_w_in, ffn1_w_out, mix_norm, ffn2_norm, ffn2_w_in, ffn2_w_out,
           a_w_qkv, a_rel_bias, a_w_o, kv_norm, kv_w_down, kv_latent_norm, kv_w_up,
           b_w_dq, b_q_norm, b_w_uq, b_w_o, final_norm):
    b, s, d = x.shape
    assert d == D_MODEL and s % ROW_TILE == 0 and s % A_QBLK == 0 and s % MLA_QBLK == 0
    t = b * s
    half = ROPE_DIM // 2
    row = lambda g: g.reshape(1, -1)

    cos64, sin64 = _rope_tables(s)
    swap = np.concatenate([np.arange(half, ROPE_DIM), np.arange(half)])

    h = x.reshape(t, d)
    h = _ffn(h, row(ffn1_norm[0]), *_ffn_weights(ffn1_w_in[0], ffn1_w_out[0]))
    q, k, v = _qkv(h, row(mix_norm[0]), a_w_qkv[0].astype(BF16))
    att = _attn_a(q.reshape(b, s, d), k.reshape(b, s, d), v.reshape(b, s, d),
                  _attn_a_bias(a_rel_bias[0]))
    h = _ffn(h, row(ffn2_norm[0]), *_ffn_weights(ffn2_w_in[0], ffn2_w_out[0]),
             attn=att.reshape(t, d), w_o=a_w_o[0].astype(BF16))

    w_rope = kv_w_down[:, KV_LORA:]
    wd = jnp.concatenate([kv_w_down[:, :KV_LORA], w_rope, w_rope,
                          w_rope[:, swap], w_rope[:, swap]], axis=1).astype(BF16)
    w_up = kv_w_up.reshape(KV_LORA, HEADS_B, NOPE_DIM + V_DIM)
    wk = w_up[:, :, :NOPE_DIM].reshape(KV_LORA, HEADS_B * NOPE_DIM).astype(BF16)
    wv = w_up[:, :, NOPE_DIM:].reshape(KV_LORA, HEADS_B * V_DIM).astype(BF16)
    kcat, v_b = _mla_kv(h, row(kv_norm), wd, row(kv_latent_norm), wk, wv,
                        jnp.tile(cos64, (1, 2)), jnp.tile(sin64, (1, 2)), s)

    h = _ffn(h, row(ffn1_norm[1]), *_ffn_weights(ffn1_w_in[1], ffn1_w_out[1]))
    w_uq = b_w_uq[0].reshape(Q_LORA, HEADS_B, NOPE_DIM + ROPE_DIM)
    uq_nope = w_uq[:, :, :NOPE_DIM].reshape(Q_LORA, HEADS_B * NOPE_DIM)
    uq_rope = w_uq[:, :, NOPE_DIM:]
    wuq = jnp.concatenate([uq_nope, uq_rope.reshape(Q_LORA, -1),
                           uq_rope[:, :, swap].reshape(Q_LORA, -1)], axis=1).astype(BF16)
    qcat = _mla_q(h, row(mix_norm[1]), b_w_dq[0].astype(BF16), row(b_q_norm[0]), wuq,
                  jnp.tile(cos64, (1, HEADS_B)), jnp.tile(sin64, (1, HEADS_B)), s)
    att = _mla_attn(qcat.reshape(b, s, 2 * d), kcat.reshape(b, s, 2 * d), v_b.reshape(b, s, d))
    h = _ffn(h, row(ffn2_norm[1]), *_ffn_weights(ffn2_w_in[1], ffn2_w_out[1]),
             attn=att.reshape(t, d), w_o=b_w_o[0].astype(BF16), final_g=row(final_norm))
    return h.reshape(b, s, d)
```

```python
import functools
from typing import Callable, NamedTuple

import jax
import jax.numpy as jnp
import numpy as np
from jax import lax
from jax.experimental import pallas as pl
from jax.experimental.pallas import tpu as pltpu

F32 = jnp.float32
BF16 = jnp.bfloat16

D_MODEL = 1024
CHUNK = 64
EPS = 1e-6
NEG_INF = -1e30
LOG2E = 1.4426950408889634

HEADS_A = 16
HEAD_DIM_A = 64
LEFT_CHUNKS = 8
MAX_REL = 128

HEADS_B = 8
NOPE_DIM = 128
ROPE_DIM = 64
V_DIM = 128
Q_LORA = 768
KV_LORA = 256
ROPE_THETA = 10000.0

D_FF = 2816
FF_CHUNK = 256
N_FF_CHUNKS = D_FF // FF_CHUNK

ROW_TILE = 512

A_QBLK = 4 * CHUNK
A_PAD = LEFT_CHUNKS * CHUNK
A_BAND = A_QBLK + A_PAD
A_DIAG = 1024
LANES = 128

MLA_QBLK = 256
MLA_SCALE = (NOPE_DIM + ROPE_DIM) ** -0.5 * LOG2E

VMEM_LIMIT = 56 * 1024 * 1024


def _cparams(n_axes):
    return pltpu.CompilerParams(
        dimension_semantics=("arbitrary",) * n_axes, vmem_limit_bytes=VMEM_LIMIT)


def _rms(x, g):
    ms = jnp.mean(x * x, axis=-1, keepdims=True)
    return x * lax.rsqrt(ms + EPS) * g


def _resident(shape):
    zeros = (0,) * len(shape)
    return pl.BlockSpec(shape, lambda *_: zeros, pipeline_mode=pl.Buffered(1))


def _rows(width):
    return pl.BlockSpec((ROW_TILE, width), lambda i: (i, 0))


class _Epilogue(NamedTuple):
    body: Callable
    args: tuple
    in_specs: tuple
    out_specs: tuple
    out_shapes: tuple


def _qkv_body(y, in_refs, out_refs):
    g_ref, w_ref = in_refs
    q_ref, k_ref, v_ref = out_refs
    hn = _rms(y, g_ref[...]).astype(BF16)
    qkv = jnp.dot(hn, w_ref[...], preferred_element_type=F32)
    q_ref[...] = (qkv[:, :D_MODEL] * (HEAD_DIM_A ** -0.5 * LOG2E)).astype(BF16)
    k_ref[...] = qkv[:, D_MODEL:2 * D_MODEL].astype(BF16)
    v_ref[...] = qkv[:, 2 * D_MODEL:].astype(BF16)


def _qkv_epilogue(t, g, w):
    out = jax.ShapeDtypeStruct((t, D_MODEL), BF16)
    return _Epilogue(_qkv_body, (g, w), (_resident(g.shape), _resident(w.shape)),
                     (_rows(D_MODEL),) * 3, (out,) * 3)


def _mla_kv_body(y, in_refs, out_refs):
    g_ref, wd_ref, gl_ref, wk_ref, wv_ref, cos_ref, sin_ref = in_refs
    knope_ref, krope_ref, v_ref = out_refs
    hn = _rms(y, g_ref[...]).astype(BF16)
    ckr = jnp.dot(hn, wd_ref[...], preferred_element_type=F32)
    c_kv = _rms(ckr[:, :KV_LORA], gl_ref[...]).astype(BF16)
    krope_ref[...] = (ckr[:, KV_LORA:KV_LORA + LANES] * cos_ref[...]
                      + ckr[:, KV_LORA + LANES:] * sin_ref[...]).astype(BF16)
    knope_ref[...] = jnp.dot(c_kv, wk_ref[...], preferred_element_type=F32).astype(BF16)
    v_ref[...] = jnp.dot(c_kv, wv_ref[...], preferred_element_type=F32).astype(BF16)


def _mla_kv_epilogue(t, seq, g, wd, gl, wk, wv, cos2, sin2):
    n_seq = seq // ROW_TILE
    tab = pl.BlockSpec((ROW_TILE, LANES), lambda i: (i % n_seq, 0))
    wide = jax.ShapeDtypeStruct((t, D_MODEL), BF16)
    return _Epilogue(
        _mla_kv_body, (g, wd, gl, wk, wv, cos2, sin2),
        (_resident(g.shape), _resident(wd.shape), _resident(gl.shape), _resident(wk.shape),
         _resident(wv.shape), tab, tab),
        (_rows(D_MODEL), _rows(LANES), _rows(D_MODEL)),
        (wide, jax.ShapeDtypeStruct((t, LANES), BF16), wide))


def _mla_q_body(y, in_refs, out_refs):
    g_ref, wdq_ref, gq_ref, wuq_ref, cos_ref, sin_ref = in_refs
    (qcat_ref,) = out_refs
    hn = _rms(y, g_ref[...]).astype(BF16)
    cq = _rms(jnp.dot(hn, wdq_ref[...], preferred_element_type=F32), gq_ref[...]).astype(BF16)
    q = jnp.dot(cq, wuq_ref[...], preferred_element_type=F32)
    n_rope = HEADS_B * ROPE_DIM
    q_nope = q[:, :D_MODEL]
    q_rope = (q[:, D_MODEL:D_MODEL + n_rope] * cos_ref[...]
              + q[:, D_MODEL + n_rope:] * sin_ref[...])
    lane = lax.broadcasted_iota(jnp.int32, (1, LANES), 1)
    low_half = lane < ROPE_DIM
    for hd in range(HEADS_B):
        pair = q_rope[:, (hd // 2) * LANES:(hd // 2 + 1) * LANES]
        keep = low_half if hd % 2 == 0 else jnp.logical_not(low_half)
        qcat_ref[:, hd * 2 * LANES:hd * 2 * LANES + LANES] = (
            q_nope[:, hd * LANES:(hd + 1) * LANES] * MLA_SCALE).astype(BF16)
        qcat_ref[:, hd * 2 * LANES + LANES:(hd + 1) * 2 * LANES] = (
            jnp.where(keep, pair, 0.0) * MLA_SCALE).astype(BF16)


def _mla_q_epilogue(t, seq, g, wdq, gq, wuq, cos8, sin8):
    n_seq = seq // ROW_TILE
    tab = pl.BlockSpec((ROW_TILE, HEADS_B * ROPE_DIM), lambda i: (i % n_seq, 0))
    return _Epilogue(
        _mla_q_body, (g, wdq, gq, wuq, cos8, sin8),
        (_resident(g.shape), _resident(wdq.shape), _resident(gq.shape), _resident(wuq.shape),
         tab, tab),
        (_rows(2 * D_MODEL),), (jax.ShapeDtypeStruct((t, 2 * D_MODEL), BF16),))


def _trunk_kernel(*refs, mixer_proj, final_norm, epilogue_body, n_epi_in):
    refs = list(refs)
    x_ref = refs.pop(0)
    if mixer_proj:
        a_ref, wo_ref = refs.pop(0), refs.pop(0)
    g_ref, win_ref, wout_ref = refs.pop(0), refs.pop(0), refs.pop(0)
    if final_norm:
        gf_ref = refs.pop(0)
    epi_in = [refs.pop(0) for _ in range(n_epi_in)]
    o_ref = refs.pop(0)
    hn_ref = refs.pop()
    epi_out = refs

    x = x_ref[...]
    if mixer_proj:
        x = x + jnp.dot(a_ref[...], wo_ref[...], preferred_element_type=F32)
        o_ref[...] = x
    hn_ref[...] = _rms(x, g_ref[...]).astype(BF16)

    hn = hn_ref[...]
    acc = None
    for c in range(N_FF_CHUNKS):
        lo = c * FF_CHUNK
        gate = jnp.dot(hn, win_ref[:, lo:lo + FF_CHUNK], preferred_element_type=F32)
        up = jnp.dot(hn, win_ref[:, D_FF + lo:D_FF + lo + FF_CHUNK], preferred_element_type=F32)
        act = (gate * jax.nn.sigmoid(gate) * up).astype(BF16)
        part = jnp.dot(act, wout_ref[lo:lo + FF_CHUNK, :], preferred_element_type=F32)
        acc = part if acc is None else acc + part
    y = (o_ref[...] if mixer_proj else x_ref[...]) + 0.5 * acc
    if final_norm:
        y = _rms(y, gf_ref[...])
    o_ref[...] = y
    if epilogue_body is not None:
        epilogue_body(o_ref[...], epi_in, epi_out)


def _trunk(name, h, g, w_in, w_out, attn=None, w_o=None, final_g=None, epilogue=None):
    t = h.shape[0]
    row = _rows(D_MODEL)
    in_specs, args = [row], [h]
    if attn is not None:
        in_specs += [row, _resident(w_o.shape)]
        args += [attn, w_o]
    in_specs += [_resident(g.shape), _resident(w_in.shape), _resident(w_out.shape)]
    args += [g, w_in, w_out]
    if final_g is not None:
        in_specs.append(_resident(final_g.shape))
        args.append(final_g)
    out_specs = [row]
    out_shapes = [jax.ShapeDtypeStruct((t, D_MODEL), F32)]
    if epilogue is not None:
        in_specs += list(epilogue.in_specs)
        args += list(epilogue.args)
        out_specs += list(epilogue.out_specs)
        out_shapes += list(epilogue.out_shapes)
    return pl.pallas_call(
        functools.partial(_trunk_kernel, mixer_proj=attn is not None,
                          final_norm=final_g is not None,
                          epilogue_body=None if epilogue is None else epilogue.body,
                          n_epi_in=0 if epilogue is None else len(epilogue.args)),
        grid=(t // ROW_TILE,),
        in_specs=in_specs,
        out_specs=out_specs,
        out_shape=out_shapes,
        scratch_shapes=[pltpu.VMEM((ROW_TILE, D_MODEL), BF16)],
        compiler_params=_cparams(1),
        name=name,
    )(*args)


def _attn_a_kernel(q_ref, k_ref, v_ref, bias_ref, o_ref):
    seq = q_ref.shape[1]
    lane = lax.broadcasted_iota(jnp.int32, (1, LANES), 1)
    low_half = lane < HEAD_DIM_A

    for blk in range(seq // A_QBLK):
        start = blk * A_QBLK
        key_lo = max(start - A_PAD, 0)
        col_lo = key_lo - (start - A_PAD)
        q = q_ref[0, start:start + A_QBLK, :]
        zero = jnp.zeros_like(q)
        q2 = jnp.concatenate([jnp.where(low_half, q, zero),
                              jnp.where(low_half, zero, q)], axis=0)
        kb = k_ref[0, key_lo:start + A_QBLK, :]
        vb = v_ref[0, key_lo:start + A_QBLK, :]
        s = lax.dot_general(q2, kb, (((1,), (1,)), ((), ())),
                            preferred_element_type=F32)
        s = s + bias_ref[0, :, col_lo:]
        m = jnp.max(s, axis=-1, keepdims=True)
        p = jnp.exp2(s - m)
        l = jnp.sum(p, axis=-1, keepdims=True)
        o2 = jnp.dot(p.astype(BF16), vb, preferred_element_type=F32) / l
        o_ref[0, start:start + A_QBLK, :] = jnp.where(
            low_half, o2[:A_QBLK], o2[A_QBLK:]).astype(BF16)


def _attn_a(q, k, v, bias):
    b, s, _ = q.shape
    n_pairs = HEADS_A // 2
    blk = pl.BlockSpec((1, s, LANES), lambda p, i: (i, 0, p))
    return pl.pallas_call(
        _attn_a_kernel,
        grid=(n_pairs, b),
        in_specs=[blk, blk, blk,
                  pl.BlockSpec((1, 2 * A_QBLK, A_BAND), lambda p, i: (p, 0, 0))],
        out_specs=blk,
        out_shape=jax.ShapeDtypeStruct((b, s, D_MODEL), BF16),
        compiler_params=_cparams(2),
        name="attn_a",
    )(q, k, v, bias.reshape(n_pairs, 2 * A_QBLK, A_BAND))


def _attn_a_bias_kernel(g_ref, o_ref):
    base = jnp.broadcast_to(g_ref[0], (A_QBLK, A_DIAG))
    toeplitz = pltpu.roll(base, 0, 1, stride=1, stride_axis=0)[:, :A_BAND]
    r = lax.broadcasted_iota(jnp.int32, (A_QBLK, A_BAND), 0) // CHUNK
    j = lax.broadcasted_iota(jnp.int32, (A_QBLK, A_BAND), 1) // CHUNK
    in_window = jnp.logical_and(j >= r, j <= r + LEFT_CHUNKS)
    o_ref[0] = jnp.where(in_window, toeplitz * LOG2E, NEG_INF)


def _attn_a_bias(rel_table):
    n_heads = rel_table.shape[0]
    far = rel_table[:, 2 * MAX_REL:]
    n_far = A_PAD - MAX_REL
    n_near = A_BAND - n_far - (2 * MAX_REL + 1)
    g = jnp.concatenate([
        jnp.broadcast_to(far, (n_heads, n_far)),
        rel_table[:, ::-1],
        jnp.broadcast_to(rel_table[:, :1], (n_heads, n_near)),
        jnp.broadcast_to(far, (n_heads, A_DIAG - A_BAND)),
    ], axis=1).astype(F32).reshape(n_heads, 1, A_DIAG)
    return pl.pallas_call(
        _attn_a_bias_kernel,
        grid=(n_heads,),
        in_specs=[pl.BlockSpec((1, 1, A_DIAG), lambda hd: (hd, 0, 0))],
        out_specs=pl.BlockSpec((1, A_QBLK, A_BAND), lambda hd: (hd, 0, 0)),
        out_shape=jax.ShapeDtypeStruct((n_heads, A_QBLK, A_BAND), F32),
        compiler_params=_cparams(1),
        name="attn_a_bias",
    )(g)


def _mla_attn_kernel(q_ref, knope_ref, krope_ref, v_ref, o_ref):
    seq = q_ref.shape[1]
    n_blk = seq // MLA_QBLK
    r = lax.broadcasted_iota(jnp.int32, (MLA_QBLK, MLA_QBLK), 0)
    c = lax.broadcasted_iota(jnp.int32, (MLA_QBLK, MLA_QBLK), 1)
    diag_ok = (c // CHUNK) <= (r // CHUNK)

    m = [None] * n_blk
    l = [None] * n_blk
    acc = [None] * n_blk
    for j in range(n_blk):
        lo = j * MLA_QBLK
        kt = jnp.concatenate([knope_ref[0, lo:lo + MLA_QBLK, :],
                              krope_ref[0, lo:lo + MLA_QBLK, :]], axis=-1)
        vt = v_ref[0, lo:lo + MLA_QBLK, :]
        s = lax.dot_general(q_ref[0, lo:, :], kt, (((1,), (1,)), ((), ())),
                            preferred_element_type=F32)
        probs, alphas = [], []
        for i in range(j, n_blk):
            si = s[(i - j) * MLA_QBLK:(i - j + 1) * MLA_QBLK]
            if i == j:
                si = jnp.where(diag_ok, si, NEG_INF)
            mi = jnp.max(si, axis=-1, keepdims=True)
            if j == 0:
                alpha = None
            else:
                mi = jnp.maximum(m[i], mi)
                alpha = jnp.exp2(m[i] - mi)
            p = jnp.exp2(si - mi)
            li = p[:, :LANES] + p[:, LANES:]
            l[i] = li if j == 0 else alpha * l[i] + li
            m[i] = mi
            probs.append(p.astype(BF16))
            alphas.append(alpha)
        pv = jnp.dot(jnp.concatenate(probs, axis=0), vt, preferred_element_type=F32)
        for i in range(j, n_blk):
            pvi = pv[(i - j) * MLA_QBLK:(i - j + 1) * MLA_QBLK]
            acc[i] = pvi if j == 0 else alphas[i - j] * acc[i] + pvi
        denom = jnp.sum(l[j], axis=-1, keepdims=True)
        o_ref[0, lo:lo + MLA_QBLK, :] = (acc[j] / denom).astype(BF16)


def _mla_attn(qcat, knope, krope, v):
    b, s, _ = v.shape
    per_head = pl.BlockSpec((1, s, LANES), lambda i, hd: (i, 0, hd))
    return pl.pallas_call(
        _mla_attn_kernel,
        grid=(b, HEADS_B),
        in_specs=[pl.BlockSpec((1, s, 2 * LANES), lambda i, hd: (i, 0, hd)), per_head,
                  pl.BlockSpec((1, s, LANES), lambda i, hd: (i, 0, 0)), per_head],
        out_specs=per_head,
        out_shape=jax.ShapeDtypeStruct((b, s, HEADS_B * V_DIM), BF16),
        compiler_params=_cparams(2),
        name="mla_attn",
    )(qcat, knope, krope, v)


def _rope_tables(seq):
    half = ROPE_DIM // 2
    freqs = ROPE_THETA ** (-jnp.arange(half, dtype=F32) / half)
    ang = jnp.arange(seq, dtype=F32)[:, None] * freqs[None, :]
    cos, sin = jnp.cos(ang), jnp.sin(ang)
    return jnp.concatenate([cos, cos], axis=-1), jnp.concatenate([-sin, sin], axis=-1)


def kernel(x, ffn1_norm, ffn1_w_in, ffn1_w_out, mix_norm, ffn2_norm, ffn2_w_in, ffn2_w_out,
           a_w_qkv, a_rel_bias, a_w_o, kv_norm, kv_w_down, kv_latent_norm, kv_w_up,
           b_w_dq, b_q_norm, b_w_uq, b_w_o, final_norm):
    b, s, d = x.shape
    assert d == D_MODEL and s % ROW_TILE == 0 and s % A_QBLK == 0 and s % MLA_QBLK == 0
    t = b * s
    half = ROPE_DIM // 2
    row = lambda g: g.reshape(1, -1)
    bf = lambda w: w.astype(BF16)

    cos64, sin64 = _rope_tables(s)
    swap = np.concatenate([np.arange(half, ROPE_DIM), np.arange(half)])

    h, q, k, v = _trunk(
        "ffn_qkv", x.reshape(t, d), row(ffn1_norm[0]), bf(ffn1_w_in[0]), bf(ffn1_w_out[0]),
        epilogue=_qkv_epilogue(t, row(mix_norm[0]), bf(a_w_qkv[0])))
    att = _attn_a(q.reshape(b, s, d), k.reshape(b, s, d), v.reshape(b, s, d),
                  _attn_a_bias(a_rel_bias[0]))

    w_rope = kv_w_down[:, KV_LORA:]
    wd = bf(jnp.concatenate([kv_w_down[:, :KV_LORA], w_rope, w_rope,
                             w_rope[:, swap], w_rope[:, swap]], axis=1))
    w_up = kv_w_up.reshape(KV_LORA, HEADS_B, NOPE_DIM + V_DIM)
    wk = bf(w_up[:, :, :NOPE_DIM].reshape(KV_LORA, HEADS_B * NOPE_DIM))
    wv = bf(w_up[:, :, NOPE_DIM:].reshape(KV_LORA, HEADS_B * V_DIM))
    h, knope, krope, v_b = _trunk(
        "ffn_kv", h, row(ffn2_norm[0]), bf(ffn2_w_in[0]), bf(ffn2_w_out[0]),
        attn=att.reshape(t, d), w_o=bf(a_w_o[0]),
        epilogue=_mla_kv_epilogue(t, s, row(kv_norm), wd, row(kv_latent_norm), wk, wv,
                                  jnp.tile(cos64, (1, 2)), jnp.tile(sin64, (1, 2))))

    w_uq = b_w_uq[0].reshape(Q_LORA, HEADS_B, NOPE_DIM + ROPE_DIM)
    uq_nope = w_uq[:, :, :NOPE_DIM].reshape(Q_LORA, HEADS_B * NOPE_DIM)
    uq_rope = w_uq[:, :, NOPE_DIM:]
    wuq = bf(jnp.concatenate([uq_nope, uq_rope.reshape(Q_LORA, -1),
                              uq_rope[:, :, swap].reshape(Q_LORA, -1)], axis=1))
    h, qcat = _trunk(
        "ffn_q", h, row(ffn1_norm[1]), bf(ffn1_w_in[1]), bf(ffn1_w_out[1]),
        epilogue=_mla_q_epilogue(t, s, row(mix_norm[1]), bf(b_w_dq[0]), row(b_q_norm[0]), wuq,
                                 jnp.tile(cos64, (1, HEADS_B)), jnp.tile(sin64, (1, HEADS_B))))
    att = _mla_attn(qcat.reshape(b, s, 2 * d), knope.reshape(b, s, d),
                    krope.reshape(b, s, LANES), v_b.reshape(b, s, d))
    (out,) = _trunk(
        "ffn_final", h, row(ffn2_norm[1]), bf(ffn2_w_in[1]), bf(ffn2_w_out[1]),
        attn=att.reshape(t, d), w_o=bf(b_w_o[0]), final_g=row(final_norm))
    return out.reshape(b, s, d)
```
